```python
import math, functools
import jax, jax.numpy as jnp
from jax import lax
import numpy as np

D_MODEL = 1024
BATCH = 8
SEQ = 4096
DEPTH = 2
DEC_BATCH = 128
DEC_SEQ = 4
PAST_LEN = 16384
PAGE_SIZE = 128

ATTN_WIDTH = D_MODEL // 2
SSM_WIDTH = D_MODEL - ATTN_WIDTH
HEAD_DIM = 64
N_HEADS = ATTN_WIDTH // HEAD_DIM
N_KV_HEADS = 2
GQA_GROUP = N_HEADS // N_KV_HEADS
KV_WIDTH = N_KV_HEADS * HEAD_DIM
WINDOW = 128
BLOCK = WINDOW
GROUP_CH = 16
N_GROUPS = SSM_WIDTH // GROUP_CH
STATE = 64
IN_WIDTH = ATTN_WIDTH + 2 * KV_WIDTH + SSM_WIDTH
D_FF = 2688
CONV_W = 3
EPS = 1e-5
DT_MIN = 1e-3
DT_MAX = 1e-1
MASK_VALUE = -1e30

kernel_name = "hymba_swa_s5_convffn_step"


def _rmsnorm(x, g):
    xf = x.astype(jnp.float32)
    y = xf * lax.rsqrt(jnp.mean(xf * xf, axis=-1, keepdims=True) + EPS)
    return (y * g.astype(jnp.float32)).astype(x.dtype)


def _alibi_slopes():
    h = jnp.arange(1, N_HEADS + 1, dtype=jnp.float32)
    return jnp.exp2(-8.0 * h / N_HEADS).reshape(N_KV_HEADS, GQA_GROUP)


def _sink_softmax(s, sink):
    sk = sink[:, :, None, None]
    m = jnp.maximum(jnp.max(s, axis=-1, keepdims=True), sk)
    e = jnp.exp(s - m)
    return e / (jnp.sum(e, axis=-1, keepdims=True) + jnp.exp(sk - m))


def _swa_prompt(q, k, v, sinks):
    b, L = q.shape[0], q.shape[1]
    nb = L // BLOCK
    scale = HEAD_DIM ** -0.5
    qb = q.reshape(b, nb, BLOCK, N_KV_HEADS, GQA_GROUP, HEAD_DIM)
    pad = jnp.zeros((b, BLOCK, N_KV_HEADS, HEAD_DIM), k.dtype)
    kb = jnp.concatenate([pad, k], axis=1).reshape(b, nb + 1, BLOCK, N_KV_HEADS, HEAD_DIM)
    vb = jnp.concatenate([pad.astype(v.dtype), v], axis=1).reshape(b, nb + 1, BLOCK, N_KV_HEADS, HEAD_DIM)
    kband = jnp.concatenate([kb[:, :-1], kb[:, 1:]], axis=2)
    vband = jnp.concatenate([vb[:, :-1], vb[:, 1:]], axis=2)
    s = jnp.einsum('bnqhgd,bnkhd->bnhgqk', qb, kband,
                   preferred_element_type=jnp.float32) * scale
    blk = jnp.arange(nb)[:, None] * BLOCK
    qpos = blk + jnp.arange(BLOCK)[None, :]
    kpos = blk - BLOCK + jnp.arange(2 * BLOCK)[None, :]
    dist = qpos[:, :, None] - kpos[:, None, :]
    valid = (dist >= 0) & (dist < WINDOW) & (kpos[:, None, :] >= 0)
    slopes = _alibi_slopes()
    s = s - slopes[None, None, :, :, None, None] * dist[None, :, None, None].astype(jnp.float32)
    s = jnp.where(valid[None, :, None, None], s, MASK_VALUE)
    p = _sink_softmax(s, sinks)
    o = jnp.einsum('bnhgqk,bnkhd->bnqhgd', p.astype(v.dtype), vband)
    w = min(WINDOW, L)
    return o.reshape(b, L, ATTN_WIDTH), k[:, -w:], v[:, -w:]


def _swa_sample(q, k, v, sinks, k_buf, v_buf):
    b, T = q.shape[0], q.shape[1]
    w = k_buf.shape[1]
    scale = HEAD_DIM ** -0.5
    kc = jnp.concatenate([k_buf.astype(k.dtype), k], axis=1)
    vc = jnp.concatenate([v_buf.astype(v.dtype), v], axis=1)
    qg = q.reshape(b, T, N_KV_HEADS, GQA_GROUP, HEAD_DIM)
    s = jnp.einsum('bqhgd,bkhd->bhgqk', qg, kc,
                   preferred_element_type=jnp.float32) * scale
    qpos = PAST_LEN + jnp.arange(T)
    kpos = jnp.concatenate([PAST_LEN - w + jnp.arange(w), PAST_LEN + jnp.arange(T)])
    dist = qpos[:, None] - kpos[None, :]
    valid = (dist >= 0) & (dist < WINDOW)
    slopes = _alibi_slopes()
    s = s - slopes[None, :, :, None, None] * dist[None, None, None].astype(jnp.float32)
    s = jnp.where(valid[None, None, None], s, MASK_VALUE)
    p = _sink_softmax(s, sinks)
    o = jnp.einsum('bhgqk,bkhd->bqhgd', p.astype(v.dtype), vc)
    return o.reshape(b, T, ATTN_WIDTH), kc[:, -w:], vc[:, -w:]


def _lin_combine(left, right):
    a1, b1 = left
    a2, b2 = right
    return a1 * a2, a2 * b1 + b2


def _s5(u, h0_re, h0_im, lam_re, lam_im, log_step, b_re, b_im, c_re, c_im, d_skip):
    f32 = jnp.float32
    b, L = u.shape[0], u.shape[1]
    ug = u.astype(f32).reshape(b, L, N_GROUPS, GROUP_CH)
    lam = lax.complex(lam_re.astype(f32), lam_im.astype(f32))
    dt = jnp.exp(log_step.astype(f32))[:, None]
    lam_bar = jnp.exp(lam * dt)
    b_bar = ((lam_bar - 1.0) / lam)[..., None] * lax.complex(b_re.astype(f32), b_im.astype(f32))
    bu = jnp.einsum('blgh,gnh->blgn', ug.astype(jnp.complex64), b_bar)
    a = jnp.broadcast_to(lam_bar, (1, L, N_GROUPS, STATE))
    a_cum, xs = lax.associative_scan(_lin_combine, (a, bu), axis=1)
    h0 = lax.complex(h0_re.astype(f32), h0_im.astype(f32))
    xs = xs + a_cum * h0[:, None]
    c = lax.complex(c_re.astype(f32), c_im.astype(f32))
    y = jnp.einsum('blgn,ghn->blgh', xs, c).real \
        + d_skip.astype(f32).reshape(N_GROUPS, GROUP_CH) * ug
    h_last = xs[:, -1]
    return y.reshape(b, L, SSM_WIDTH).astype(u.dtype), jnp.real(h_last), jnp.imag(h_last)


def _conv_ffn(h, prev, w_up, conv_w, conv_b, w_down):
    L = h.shape[1]
    up = h @ w_up
    buf = jnp.concatenate([prev.astype(up.dtype), up], axis=1)
    c = conv_b + sum(conv_w[i] * buf[:, i:i + L] for i in range(CONV_W))
    a, g = jnp.split(c, 2, axis=-1)
    return (jax.nn.silu(g) * a) @ w_down, buf[:, -(CONV_W - 1):]


def _layer(x, p, attend, h0_re, h0_im, conv_prev):
    b, L = x.shape[0], x.shape[1]
    h = _rmsnorm(x, p['norm_mix'])
    proj = h @ p['w_in']
    q = proj[..., :ATTN_WIDTH].reshape(b, L, N_HEADS, HEAD_DIM)
    k = proj[..., ATTN_WIDTH:ATTN_WIDTH + KV_WIDTH].reshape(b, L, N_KV_HEADS, HEAD_DIM)
    v = proj[..., ATTN_WIDTH + KV_WIDTH:ATTN_WIDTH + 2 * KV_WIDTH].reshape(b, L, N_KV_HEADS, HEAD_DIM)
    u = proj[..., ATTN_WIDTH + 2 * KV_WIDTH:]
    sinks = p['sinks'].astype(jnp.float32).reshape(N_KV_HEADS, GQA_GROUP)
    attn, k_win, v_win = attend(q, k, v, sinks)
    ssm, h_re, h_im = _s5(u, h0_re, h0_im, p['lam_re'], p['lam_im'], p['log_step'],
                          p['b_re'], p['b_im'], p['c_re'], p['c_im'], p['d_skip'])
    ssm = jax.nn.gelu(ssm)
    ssm = ssm * jax.nn.sigmoid(ssm @ p['w_glu'] + p['b_glu'])
    merged = jnp.concatenate([_rmsnorm(attn, p['g_attn']), _rmsnorm(ssm, p['g_ssm'])], axis=-1)
    x = x + merged @ p['w_out']
    f, conv_new = _conv_ffn(_rmsnorm(x, p['norm_ffn']), conv_prev,
                            p['w_up'], p['conv_w'], p['conv_b'], p['w_down'])
    x = x + f
    return x, (k_win, v_win, h_re, h_im, conv_new)


def setup_inputs(seed: int = 0) -> dict:
    key = jax.random.key(seed)
    ks = jax.random.split(key, 32)
    f32 = jnp.float32
    win_buf = min(WINDOW, PAST_LEN)

    def nrm(k, shape, scale):
        return jax.random.normal(k, shape, f32) * scale

    n_idx = jnp.arange(STATE, dtype=f32)
    lam_re = -0.5 + nrm(ks[10], (DEPTH, N_GROUPS, STATE), 0.01)
    lam_im = math.pi * n_idx[None, None, :] + nrm(ks[11], (DEPTH, N_GROUPS, STATE), 0.01)
    log_step = jax.random.uniform(ks[12], (DEPTH, N_GROUPS), f32,
                                  math.log(DT_MIN), math.log(DT_MAX))
    return {
        "x_prompt": nrm(ks[0], (BATCH, SEQ, D_MODEL), 1.0),
        "x_sample": nrm(ks[1], (DEC_BATCH, DEC_SEQ, D_MODEL), 1.0),
        "cache_k_win": nrm(ks[2], (DEPTH, DEC_BATCH, win_buf, N_KV_HEADS, HEAD_DIM), 1.0),
        "cache_v_win": nrm(ks[3], (DEPTH, DEC_BATCH, win_buf, N_KV_HEADS, HEAD_DIM), 1.0),
        "state_ssm_re": nrm(ks[4], (DEPTH, DEC_BATCH, N_GROUPS, STATE), 0.5),
        "state_ssm_im": nrm(ks[5], (DEPTH, DEC_BATCH, N_GROUPS, STATE), 0.5),
        "state_conv": nrm(ks[6], (DEPTH, DEC_BATCH, CONV_W - 1, 2 * D_FF), 1.0),
        "norm_mix": 1.0 + nrm(ks[7], (DEPTH, D_MODEL), 0.02),
        "w_in": nrm(ks[8], (DEPTH, D_MODEL, IN_WIDTH), D_MODEL ** -0.5),
        "sinks": nrm(ks[9], (DEPTH, N_HEADS), 0.5),
        "lam_re": lam_re,
        "lam_im": lam_im,
        "log_step": log_step,
        "b_re": nrm(ks[13], (DEPTH, N_GROUPS, STATE, GROUP_CH), (2 * GROUP_CH) ** -0.5),
        "b_im": nrm(ks[14], (DEPTH, N_GROUPS, STATE, GROUP_CH), (2 * GROUP_CH) ** -0.5),
        "c_re": nrm(ks[15], (DEPTH, N_GROUPS, GROUP_CH, STATE), (2 * STATE) ** -0.5),
        "c_im": nrm(ks[16], (DEPTH, N_GROUPS, GROUP_CH, STATE), (2 * STATE) ** -0.5),
        "d_skip": nrm(ks[17], (DEPTH, SSM_WIDTH), 1.0),
        "w_glu": nrm(ks[18], (DEPTH, SSM_WIDTH, SSM_WIDTH), SSM_WIDTH ** -0.5),
        "b_glu": nrm(ks[19], (DEPTH, SSM_WIDTH), 0.01),
        "g_attn": 1.0 + nrm(ks[20], (DEPTH, ATTN_WIDTH), 0.02),
        "g_ssm": 1.0 + nrm(ks[21], (DEPTH, SSM_WIDTH), 0.02),
        "w_out": nrm(ks[22], (DEPTH, ATTN_WIDTH + SSM_WIDTH, D_MODEL), (ATTN_WIDTH + SSM_WIDTH) ** -0.5),
        "norm_ffn": 1.0 + nrm(ks[23], (DEPTH, D_MODEL), 0.02),
        "w_up": nrm(ks[24], (DEPTH, D_MODEL, 2 * D_FF), D_MODEL ** -0.5),
        "conv_w": nrm(ks[25], (DEPTH, CONV_W, 2 * D_FF), CONV_W ** -0.5),
        "conv_b": nrm(ks[26], (DEPTH, 2 * D_FF), 0.01),
        "w_down": nrm(ks[27], (DEPTH, D_FF, D_MODEL), D_FF ** -0.5),
        "norm_final": 1.0 + nrm(ks[28], (D_MODEL,), 0.02),
    }


def reference(x_prompt, x_sample, cache_k_win, cache_v_win, state_ssm_re, state_ssm_im, state_conv,
              norm_mix, w_in, sinks, lam_re, lam_im, log_step, b_re, b_im, c_re, c_im, d_skip,
              w_glu, b_glu, g_attn, g_ssm, w_out, norm_ffn, w_up, conv_w, conv_b, w_down, norm_final):
    xp, xs = x_prompt, x_sample
    bp = x_prompt.shape[0]
    zero_h = jnp.zeros((bp, N_GROUPS, STATE), jnp.float32)
    zero_conv = jnp.zeros((bp, CONV_W - 1, 2 * D_FF), x_prompt.dtype)
    prompt_states, sample_states = [], []
    for l in range(DEPTH):
        p = {
            'norm_mix': norm_mix[l], 'w_in': w_in[l], 'sinks': sinks[l],
            'lam_re': lam_re[l], 'lam_im': lam_im[l], 'log_step': log_step[l],
            'b_re': b_re[l], 'b_im': b_im[l], 'c_re': c_re[l], 'c_im': c_im[l],
            'd_skip': d_skip[l], 'w_glu': w_glu[l], 'b_glu': b_glu[l],
            'g_attn': g_attn[l], 'g_ssm': g_ssm[l], 'w_out': w_out[l],
            'norm_ffn': norm_ffn[l], 'w_up': w_up[l], 'conv_w': conv_w[l],
            'conv_b': conv_b[l], 'w_down': w_down[l],
        }
        xp, sp = _layer(xp, p, _swa_prompt, zero_h, zero_h, zero_conv)
        attend_s = functools.partial(_swa_sample, k_buf=cache_k_win[l], v_buf=cache_v_win[l])
        xs, ss = _layer(xs, p, attend_s, state_ssm_re[l], state_ssm_im[l], state_conv[l])
        prompt_states.append(sp)
        sample_states.append(ss)
    y_prompt = _rmsnorm(xp, norm_final)
    y_sample = _rmsnorm(xs, norm_final)
    k_win_p, v_win_p, ssm_re_p, ssm_im_p, conv_p = (jnp.stack(z) for z in zip(*prompt_states))
    k_win_s, v_win_s, ssm_re_s, ssm_im_s, conv_s = (jnp.stack(z) for z in zip(*sample_states))
    return (y_prompt, y_sample, k_win_p, v_win_p, ssm_re_p, ssm_im_p, conv_p,
            k_win_s, v_win_s, ssm_re_s, ssm_im_s, conv_s)
```

```python
import functools
import math

import numpy as np
import jax
import jax.numpy as jnp
from jax import lax
from jax.experimental import pallas as pl
from jax.experimental.pallas import tpu as pltpu

F32 = jnp.float32
BF16 = jnp.bfloat16

D_MODEL = 1024
ATTN_WIDTH = 512
SSM_WIDTH = 512
HEAD_DIM = 64
N_HEADS = 8
N_KV_HEADS = 2
GQA_GROUP = 4
KV_WIDTH = N_KV_HEADS * HEAD_DIM
WINDOW = 128
GROUP_CH = 16
N_GROUPS = 32
STATE = 64
IN_WIDTH = ATTN_WIDTH + 2 * KV_WIDTH + SSM_WIDTH
D_FF = 2688
CONV_W = 3
EPS = 1e-5
MASK_VALUE = -1e30

LANES = 128
SUBLANES = 8
CHUNK = SUBLANES
N_SETS = SSM_WIDTH // LANES
SET_GROUPS = LANES // GROUP_CH
SET_STATE = SET_GROUPS * STATE
Q_OFF, K_OFF, V_OFF, U_OFF = 0, ATTN_WIDTH, ATTN_WIDTH + KV_WIDTH, ATTN_WIDTH + 2 * KV_WIDTH
FFN_CHUNKS = ((0, 1024), (1024, 1024), (2048, 640))
VMEM_LIMIT = 56 * 1024 * 1024


def _rms(x, g):
    ms = jnp.mean(x * x, axis=-1, keepdims=True)
    return x * lax.rsqrt(ms + EPS) * g


def _dot(a, b):
    return jnp.dot(a, b, preferred_element_type=F32)


def _dot_nt(a, b):
    return lax.dot_general(a, b, (((1,), (1,)), ((), ())), preferred_element_type=F32)


def _const_spec(shape):
    nd = len(shape)
    return pl.BlockSpec(shape, lambda *_: (0,) * nd, pipeline_mode=pl.Buffered(1))


def _half_pads(t):
    lo = lax.broadcasted_iota(jnp.int32, t.shape, 1) < HEAD_DIM
    r = pltpu.roll(t, HEAD_DIM, axis=1)
    z = jnp.zeros_like(t)
    return ((jnp.where(lo, t, z), jnp.where(lo, z, r)),
            (jnp.where(lo, r, z), jnp.where(lo, z, t)))


def _cat_bf16(tiles, axis=0):
    return jnp.concatenate(tiles, axis=axis).astype(BF16)


def _prompt_bias():
    qi = np.arange(WINDOW)[:, None]
    kj = np.arange(2 * WINDOW)[None, :]
    dist = (qi + WINDOW - kj).astype(np.float64)
    valid = (dist >= 0) & (dist < WINDOW)
    out = np.zeros((2, N_KV_HEADS, 2 * WINDOW, 4 * WINDOW), np.float32)
    for var in range(2):
        v = valid & (kj >= WINDOW) if var == 1 else valid
        for kvh in range(N_KV_HEADS):
            for ti in range(2):
                for half in range(2):
                    hh = GQA_GROUP * kvh + 2 * ti + half
                    slope = 2.0 ** (-(hh + 1))
                    b = np.where(v, -slope * dist, MASK_VALUE)
                    out[var, kvh, ti * WINDOW:(ti + 1) * WINDOW,
                        half * 2 * WINDOW:(half + 1) * 2 * WINDOW] = b
    return out


def _sample_bias(t_new):
    t = np.arange(SUBLANES)[:, None].astype(np.float64)
    j = np.arange(WINDOW)[None, :]
    tn = np.arange(SUBLANES)[None, :]
    bc = np.zeros((N_KV_HEADS, 2 * SUBLANES, 2 * WINDOW), np.float32)
    bn = np.zeros((N_KV_HEADS, 2 * SUBLANES, 2 * SUBLANES), np.float32)
    for kvh in range(N_KV_HEADS):
        for ti in range(2):
            for half in range(2):
                hh = GQA_GROUP * kvh + 2 * ti + half
                slope = 2.0 ** (-(hh + 1))
                dc = t + WINDOW - j
                vc = dc < WINDOW
                bc[kvh, ti * 8:(ti + 1) * 8, half * WINDOW:(half + 1) * WINDOW] = np.where(vc, -slope * dc, MASK_VALUE)
                dn = t - tn
                vn = (dn >= 0) & (tn < t_new)
                bn[kvh, ti * 8:(ti + 1) * 8, half * 8:(half + 1) * 8] = np.where(vn, -slope * dn, MASK_VALUE)
    return bc, bn


def _mix_in_prompt_kernel(sinks_ref, x_ref, nm_ref, win_ref, bias_ref, ga_ref,
                          attn_ref, u_ref, kwin_ref, vwin_ref, kprev, vprev, *, tq):
    i = pl.program_id(1)
    nb = tq // WINDOW

    @pl.when(i == 0)
    def _():
        kprev[...] = jnp.zeros_like(kprev)
        vprev[...] = jnp.zeros_like(vprev)

    hn = _rms(x_ref[0], nm_ref[...]).astype(BF16)
    proj = _dot(hn, win_ref[...])
    u_ref[0] = proj[:, U_OFF:]

    lo = lax.broadcasted_iota(jnp.int32, (WINDOW, LANES), 1) < HEAD_DIM
    kp = _half_pads(kprev[...])
    vp = _half_pads(vprev[...])
    for j in range(nb):
        r0 = j * WINDOW
        kc = _half_pads(proj[r0:r0 + WINDOW, K_OFF:K_OFF + LANES])
        vc = _half_pads(proj[r0:r0 + WINDOW, V_OFF:V_OFF + LANES])
        tiles = []
        for kvh in range(N_KV_HEADS):
            q0 = Q_OFF + 2 * kvh * LANES
            lq = jnp.concatenate([proj[r0:r0 + WINDOW, q0:q0 + LANES],
                                  proj[r0:r0 + WINDOW, q0 + LANES:q0 + 2 * LANES]], axis=0)
            lq = (lq * (HEAD_DIM ** -0.5)).astype(BF16)
            kcat = _cat_bf16([kp[kvh][0], kc[kvh][0], kp[kvh][1], kc[kvh][1]])
            vcat = _cat_bf16([vp[kvh][0], vc[kvh][0], vp[kvh][1], vc[kvh][1]])
            if j == 0:
                bias = bias_ref[jnp.where(i == 0, 1, 0), kvh]
            else:
                bias = bias_ref[0, kvh]
            s = _dot_nt(lq, kcat) + bias
            e_rows, scales = [], []
            for ti in range(2):
                e_cols, rden = [], []
                for half in range(2):
                    sink = sinks_ref[GQA_GROUP * kvh + 2 * ti + half]
                    slab = s[ti * WINDOW:(ti + 1) * WINDOW, half * 2 * WINDOW:(half + 1) * 2 * WINDOW]
                    m = jnp.maximum(jnp.max(slab, axis=-1, keepdims=True), sink)
                    e = jnp.exp(slab - m)
                    den = jnp.sum(e, axis=-1, keepdims=True) + jnp.exp(sink - m)
                    e_cols.append(e.astype(BF16))
                    rden.append(1.0 / den)
                e_rows.append(jnp.concatenate(e_cols, axis=1))
                scales.append(jnp.where(lo, rden[0], rden[1]))
            o = _dot(jnp.concatenate(e_rows, axis=0), vcat)
            for ti in range(2):
                tiles.append(o[ti * WINDOW:(ti + 1) * WINDOW] * scales[ti])
        attn = jnp.concatenate(tiles, axis=1)
        attn_ref[0, r0:r0 + WINDOW, :] = _rms(attn, ga_ref[...]).astype(BF16)
        kp, vp = kc, vc

    kprev[...] = proj[tq - WINDOW:, K_OFF:K_OFF + LANES]
    vprev[...] = proj[tq - WINDOW:, V_OFF:V_OFF + LANES]

    @pl.when(i == pl.num_programs(1) - 1)
    def _():
        kwin_ref[0] = proj[tq - WINDOW:, K_OFF:K_OFF + LANES]
        vwin_ref[0] = proj[tq - WINDOW:, V_OFF:V_OFF + LANES]


def _mix_in_prompt(x, p, tq):
    b, L, _ = x.shape
    grid = (b, L // tq)
    bias = jnp.asarray(_prompt_bias())
    return pl.pallas_call(
        functools.partial(_mix_in_prompt_kernel, tq=tq),
        grid_spec=pltpu.PrefetchScalarGridSpec(
            num_scalar_prefetch=1,
            grid=grid,
            in_specs=[
                pl.BlockSpec((1, tq, D_MODEL), lambda bi, i, s: (bi, i, 0)),
                _const_spec((1, D_MODEL)),
                _const_spec((D_MODEL, IN_WIDTH)),
                _const_spec(bias.shape),
                _const_spec((1, ATTN_WIDTH)),
            ],
            out_specs=[
                pl.BlockSpec((1, tq, ATTN_WIDTH), lambda bi, i, s: (bi, i, 0)),
                pl.BlockSpec((1, tq, SSM_WIDTH), lambda bi, i, s: (bi, i, 0)),
                pl.BlockSpec((1, WINDOW, KV_WIDTH), lambda bi, i, s: (bi, 0, 0)),
                pl.BlockSpec((1, WINDOW, KV_WIDTH), lambda bi, i, s: (bi, 0, 0)),
            ],
            scratch_shapes=[pltpu.VMEM((WINDOW, KV_WIDTH), F32),
                            pltpu.VMEM((WINDOW, KV_WIDTH), F32)],
        ),
        out_shape=[
            jax.ShapeDtypeStruct((b, L, ATTN_WIDTH), BF16),
            jax.ShapeDtypeStruct((b, L, SSM_WIDTH), F32),
            jax.ShapeDtypeStruct((b, WINDOW, KV_WIDTH), F32),
            jax.ShapeDtypeStruct((b, WINDOW, KV_WIDTH), F32),
        ],
        compiler_params=pltpu.CompilerParams(
            dimension_semantics=("arbitrary", "arbitrary"), vmem_limit_bytes=VMEM_LIMIT),
        name="mix_in_prompt",
    )(p['sinks'], x, p['norm_mix'], p['w_in'], bias, p['g_attn'])


def _mix_in_sample_kernel(sinks_ref, x_ref, nm_ref, win_ref, bc_ref, bn_ref, ga_ref, kc_ref, vc_ref,
                          attn_ref, u_ref, kwin_ref, vwin_ref, proj_scr, attn_scr, *, bb, t_new):
    hn = _rms(x_ref[...], nm_ref[...]).astype(BF16)
    proj_scr[...] = _dot(hn, win_ref[...])
    u_ref[...] = proj_scr[:, U_OFF:]
    lo = lax.broadcasted_iota(jnp.int32, (SUBLANES, LANES), 1) < HEAD_DIM
    row8 = lax.broadcasted_iota(jnp.int32, (SUBLANES, LANES), 0)

    def body(bi, carry):
        r0 = pl.multiple_of(bi * SUBLANES, SUBLANES)
        pr = proj_scr[pl.ds(r0, SUBLANES), :]
        kcache = kc_ref[bi]
        vcache = vc_ref[bi]
        knew = pr[:, K_OFF:K_OFF + LANES]
        vnew = pr[:, V_OFF:V_OFF + LANES]
        kcp, vcp = _half_pads(kcache), _half_pads(vcache)
        knp, vnp = _half_pads(knew), _half_pads(vnew)
        tiles = []
        for kvh in range(N_KV_HEADS):
            q0 = Q_OFF + 2 * kvh * LANES
            lq = jnp.concatenate([pr[:, q0:q0 + LANES], pr[:, q0 + LANES:q0 + 2 * LANES]], axis=0)
            lq = (lq * (HEAD_DIM ** -0.5)).astype(BF16)
            sc = _dot_nt(lq, _cat_bf16(kcp[kvh])) + bc_ref[kvh]
            sn = _dot_nt(lq, _cat_bf16(knp[kvh])) + bn_ref[kvh]
            ec_rows, en_rows, scales = [], [], []
            for ti in range(2):
                ec_cols, en_cols, rden = [], [], []
                for half in range(2):
                    sink = sinks_ref[GQA_GROUP * kvh + 2 * ti + half]
                    slab_c = sc[ti * 8:(ti + 1) * 8, half * WINDOW:(half + 1) * WINDOW]
                    slab_n = sn[ti * 8:(ti + 1) * 8, half * 8:(half + 1) * 8]
                    m = jnp.maximum(jnp.maximum(jnp.max(slab_c, axis=-1, keepdims=True),
                                                jnp.max(slab_n, axis=-1, keepdims=True)), sink)
                    e_c = jnp.exp(slab_c - m)
                    e_n = jnp.exp(slab_n - m)
                    den = (jnp.sum(e_c, axis=-1, keepdims=True) + jnp.sum(e_n, axis=-1, keepdims=True)
                           + jnp.exp(sink - m))
                    ec_cols.append(e_c)
                    en_cols.append(e_n)
                    rden.append(1.0 / den)
                ec_rows.append(jnp.concatenate(ec_cols, axis=1))
                en_rows.append(jnp.concatenate(en_cols, axis=1))
                scales.append(jnp.where(lo, rden[0], rden[1]))
            o = (_dot(_cat_bf16(ec_rows), _cat_bf16(vcp[kvh]))
                 + _dot(_cat_bf16(en_rows), _cat_bf16(vnp[kvh])))
            for ti in range(2):
                tiles.append(o[ti * 8:(ti + 1) * 8] * scales[ti])
        attn_scr[pl.ds(r0, SUBLANES), :] = jnp.concatenate(tiles, axis=1)

        shift = WINDOW - t_new
        for src, new, dst in ((kcache, knew, kwin_ref), (vcache, vnew, vwin_ref)):
            rolled = pltpu.roll(src, shift, axis=0)
            tail = jnp.where(row8 >= SUBLANES - t_new,
                             pltpu.roll(new, SUBLANES - t_new, axis=0), rolled[WINDOW - SUBLANES:])
            dst[bi, :WINDOW - SUBLANES, :] = rolled[:WINDOW - SUBLANES]
            dst[bi, WINDOW - SUBLANES:, :] = tail
        return carry

    lax.fori_loop(0, bb, body, 0)
    attn_ref[...] = _rms(attn_scr[...], ga_ref[...]).astype(BF16)


def _mix_in_sample(x, p, kcache, vcache, bb, t_new):
    rows = x.shape[0]
    nbatch = rows // SUBLANES
    grid = (nbatch // bb,)
    bc, bn = _sample_bias(t_new)
    bc, bn = jnp.asarray(bc), jnp.asarray(bn)
    tr = bb * SUBLANES
    return pl.pallas_call(
        functools.partial(_mix_in_sample_kernel, bb=bb, t_new=t_new),
        grid_spec=pltpu.PrefetchScalarGridSpec(
            num_scalar_prefetch=1,
            grid=grid,
            in_specs=[
                pl.BlockSpec((tr, D_MODEL), lambda i, s: (i, 0)),
                _const_spec((1, D_MODEL)),
                _const_spec((D_MODEL, IN_WIDTH)),
                _const_spec(bc.shape),
                _const_spec(bn.shape),
                _const_spec((1, ATTN_WIDTH)),
                pl.BlockSpec((bb, WINDOW, KV_WIDTH), lambda i, s: (i, 0, 0)),
                pl.BlockSpec((bb, WINDOW, KV_WIDTH), lambda i, s: (i, 0, 0)),
            ],
            out_specs=[
                pl.BlockSpec((tr, ATTN_WIDTH), lambda i, s: (i, 0)),
                pl.BlockSpec((tr, SSM_WIDTH), lambda i, s: (i, 0)),
                pl.BlockSpec((bb, WINDOW, KV_WIDTH), lambda i, s: (i, 0, 0)),
                pl.BlockSpec((bb, WINDOW, KV_WIDTH), lambda i, s: (i, 0, 0)),
            ],
            scratch_shapes=[pltpu.VMEM((tr, IN_WIDTH), F32), pltpu.VMEM((tr, ATTN_WIDTH), F32)],
        ),
        out_shape=[
            jax.ShapeDtypeStruct((rows, ATTN_WIDTH), BF16),
            jax.ShapeDtypeStruct((rows, SSM_WIDTH), F32),
            jax.ShapeDtypeStruct((nbatch, WINDOW, KV_WIDTH), F32),
            jax.ShapeDtypeStruct((nbatch, WINDOW, KV_WIDTH), F32),
        ],
        compiler_params=pltpu.CompilerParams(
            dimension_semantics=("arbitrary",), vmem_limit_bytes=VMEM_LIMIT),
        name="mix_in_sample",
    )(p['sinks'], x, p['norm_mix'], p['w_in'], bc, bn, p['g_attn'], kcache, vcache)


def _s5_operators(lam_re, lam_im, log_step, b_re, b_im, c_re, c_im, d_skip):
    dt = jnp.exp(log_step)[:, None]
    k = jnp.arange(CHUNK + 1, dtype=F32)[:, None, None]
    mag = jnp.exp(k * (lam_re * dt)[None])
    ang = k * (lam_im * dt)[None]
    p_re, p_im = mag * jnp.cos(ang), mag * jnp.sin(ang)
    nr, ni = p_re[1] - 1.0, p_im[1]
    den = lam_re * lam_re + lam_im * lam_im
    co_re = (nr * lam_re + ni * lam_im) / den
    co_im = (ni * lam_re - nr * lam_im) / den
    bb_re = co_re[..., None] * b_re - co_im[..., None] * b_im
    bb_im = co_re[..., None] * b_im + co_im[..., None] * b_re
    pb_re = p_re[..., None] * bb_re[None] - p_im[..., None] * bb_im[None]
    pb_im = p_re[..., None] * bb_im[None] + p_im[..., None] * bb_re[None]
    pc_re = c_re[None] * p_re[:, :, None, :] - c_im[None] * p_im[:, :, None, :]
    pc_im = c_re[None] * p_im[:, :, None, :] + c_im[None] * p_re[:, :, None, :]
    kern = (jnp.einsum('gon,tgni->tgoi', c_re, pb_re[:CHUNK])
            - jnp.einsum('gon,tgni->tgoi', c_im, pb_im[:CHUNK]))
    s_idx = np.arange(CHUNK)[:, None]
    t_idx = np.arange(CHUNK)[None, :]
    lag = t_idx - s_idx
    t4 = jnp.where((lag >= 0)[:, :, None, None, None], kern[np.clip(lag, 0, CHUNK - 1)], 0.0)
    eye = jnp.eye(SET_GROUPS, dtype=F32)
    t4 = t4.reshape(CHUNK, CHUNK, N_SETS, SET_GROUPS, GROUP_CH, GROUP_CH)
    toep = jnp.einsum('stqgoi,gf->qsgitfo', t4, eye).reshape(N_SETS, CHUNK * LANES, CHUNK * LANES)

    def to_state(pb):
        m = pb[:CHUNK][::-1].reshape(CHUNK, N_SETS, SET_GROUPS, STATE, GROUP_CH)
        return jnp.einsum('sqgni,gf->qsgifn', m, eye).reshape(N_SETS, CHUNK * LANES, SET_STATE)

    def from_state(pc):
        m = pc[1:].reshape(CHUNK, N_SETS, SET_GROUPS, GROUP_CH, STATE)
        return jnp.einsum('tqgon,gf->qgntfo', m, eye).reshape(N_SETS, SET_STATE, CHUNK * LANES)

    def decay(x):
        return x[1:].reshape(CHUNK, N_SETS, SET_STATE).transpose(1, 0, 2)

    d_flat = jnp.tile(d_skip.reshape(N_SETS, 1, LANES), (1, 1, CHUNK))
    return dict(
        toep=toep.astype(BF16),
        wst_re=to_state(pb_re).astype(BF16), wst_im=to_state(pb_im).astype(BF16),
        wint_re=from_state(pc_re).astype(BF16), wint_im=from_state(-pc_im).astype(BF16),
        step_re=decay(p_re), step_im=decay(p_im),
        d_flat=d_flat)


def _chunk_decay_powers(lam_re, lam_im, log_step):
    dt = jnp.exp(log_step)[:, None]
    k = (CHUNK * jnp.arange(1, SUBLANES + 1, dtype=F32))[:, None, None]
    mag = jnp.exp(k * (lam_re * dt)[None])
    ang = k * (lam_im * dt)[None]
    f = lambda x: x.reshape(SUBLANES, N_SETS, SET_STATE).transpose(1, 0, 2)
    return f(mag * jnp.cos(ang)), f(mag * jnp.sin(ang))


def _s5_prompt_kernel(u_ref, toep_ref, wsr_ref, wsi_ref, wir_ref, wii_ref, pwr_ref, pwi_ref, d_ref,
                      y_ref, hre_ref, him_ref, cre, cim, hin_re, hin_im, *, nc):
    uf = jnp.concatenate([u_ref[0, pl.ds(r, nc, stride=CHUNK), :] for r in range(CHUNK)], axis=1)
    ub = uf.astype(BF16)
    cre[...] = _dot(ub, wsr_ref[0])
    cim[...] = _dot(ub, wsi_ref[0])

    row = lax.broadcasted_iota(jnp.int32, (SUBLANES, SET_STATE), 0)
    pwr, pwi = pwr_ref[0], pwi_ref[0]
    steps = []
    for k in (1, 2, 4):
        ar = jnp.where(row >= k, jnp.broadcast_to(pwr[k - 1:k], row.shape), 0.0)
        ai = jnp.where(row >= k, jnp.broadcast_to(pwi[k - 1:k], row.shape), 0.0)
        steps.append((k, ar, ai))

    def body(g, carry):
        cr, ci = carry
        r0 = pl.multiple_of(g * SUBLANES, SUBLANES)
        xr = cre[pl.ds(r0, SUBLANES), :]
        xi = cim[pl.ds(r0, SUBLANES), :]
        for k, ar, ai in steps:
            sr = pltpu.roll(xr, k, axis=0)
            si = pltpu.roll(xi, k, axis=0)
            xr, xi = xr + ar * sr - ai * si, xi + ar * si + ai * sr
        xr = xr + pwr * cr - pwi * ci
        xi = xi + pwr * ci + pwi * cr
        hin_re[pl.ds(r0, SUBLANES), :] = jnp.where(row == 0, cr, pltpu.roll(xr, 1, axis=0))
        hin_im[pl.ds(r0, SUBLANES), :] = jnp.where(row == 0, ci, pltpu.roll(xi, 1, axis=0))
        return xr[SUBLANES - 1:], xi[SUBLANES - 1:]

    zero = jnp.zeros((1, SET_STATE), F32)
    cr, ci = lax.fori_loop(0, nc // SUBLANES, body, (zero, zero))
    hre_ref[0] = cr
    him_ref[0] = ci

    y = (_dot(ub, toep_ref[0]) + _dot(hin_re[...].astype(BF16), wir_ref[0])
         + _dot(hin_im[...].astype(BF16), wii_ref[0]) + d_ref[0] * uf)
    for t in range(CHUNK):
        y_ref[0, pl.ds(t, nc, stride=CHUNK), :] = y[:, t * LANES:(t + 1) * LANES]


def _s5_prompt(u, ops, pw_re, pw_im):
    b, L, _ = u.shape
    nc = L // CHUNK
    cl = CHUNK * LANES
    set_spec = lambda r, c: pl.BlockSpec((1, r, c), lambda q, bi: (q, 0, 0))
    return pl.pallas_call(
        functools.partial(_s5_prompt_kernel, nc=nc),
        grid=(N_SETS, b),
        in_specs=[
            pl.BlockSpec((1, L, LANES), lambda q, bi: (bi, 0, q)),
            set_spec(cl, cl), set_spec(cl, SET_STATE), set_spec(cl, SET_STATE),
            set_spec(SET_STATE, cl), set_spec(SET_STATE, cl),
            set_spec(SUBLANES, SET_STATE), set_spec(SUBLANES, SET_STATE), set_spec(1, cl),
        ],
        out_specs=[
            pl.BlockSpec((1, L, LANES), lambda q, bi: (bi, 0, q)),
            pl.BlockSpec((1, 1, SET_STATE), lambda q, bi: (bi, 0, q)),
            pl.BlockSpec((1, 1, SET_STATE), lambda q, bi: (bi, 0, q)),
        ],
        out_shape=[
            jax.ShapeDtypeStruct((b, L, SSM_WIDTH), F32),
            jax.ShapeDtypeStruct((b, 1, N_GROUPS * STATE), F32),
            jax.ShapeDtypeStruct((b, 1, N_GROUPS * STATE), F32),
        ],
        scratch_shapes=[pltpu.VMEM((nc, SET_STATE), F32) for _ in range(4)],
        compiler_params=pltpu.CompilerParams(
            dimension_semantics=("arbitrary", "arbitrary"), vmem_limit_bytes=VMEM_LIMIT),
        name="s5_prompt",
    )(u, ops['toep'], ops['wst_re'], ops['wst_im'], ops['wint_re'], ops['wint_im'],
      pw_re, pw_im, ops['d_flat'])


def _s5_sample_kernel(u_ref, h0r_ref, h0i_ref, toep_ref, wsr_ref, wsi_ref, wir_ref, wii_ref,
                      ar_ref, ai_ref, d_ref, y_ref, hre_ref, him_ref, *, nbatch, t_new):
    uf = jnp.concatenate([u_ref[pl.ds(r, nbatch, stride=SUBLANES), :] for r in range(t_new)], axis=1)
    ub = uf.astype(BF16)
    h0r, h0i = h0r_ref[...], h0i_ref[...]
    ar, ai = ar_ref[0], ai_ref[0]
    hre_ref[...] = ar * h0r - ai * h0i + _dot(ub, wsr_ref[0])
    him_ref[...] = ar * h0i + ai * h0r + _dot(ub, wsi_ref[0])
    y = (_dot(ub, toep_ref[0]) + _dot(h0r.astype(BF16), wir_ref[0])
         + _dot(h0i.astype(BF16), wii_ref[0]) + d_ref[0] * uf)
    y_ref[...] = jnp.zeros_like(y_ref)
    for t in range(t_new):
        y_ref[pl.ds(t, nbatch, stride=SUBLANES), :] = y[:, t * LANES:(t + 1) * LANES]


def _s5_sample(u, h0_re, h0_im, ops, t_new):
    rows = u.shape[0]
    nbatch = rows // SUBLANES
    tl = t_new * LANES
    lo = (CHUNK - t_new) * LANES
    toep = ops['toep'][:, :tl, :tl]
    wsr, wsi = ops['wst_re'][:, lo:], ops['wst_im'][:, lo:]
    wir, wii = ops['wint_re'][:, :, :tl], ops['wint_im'][:, :, :tl]
    ar = ops['step_re'][:, t_new - 1:t_new]
    ai = ops['step_im'][:, t_new - 1:t_new]
    d = ops['d_flat'][:, :, :tl]
    set_spec = lambda r, c: pl.BlockSpec((1, r, c), lambda q: (q, 0, 0))
    return pl.pallas_call(
        functools.partial(_s5_sample_kernel, nbatch=nbatch, t_new=t_new),
        grid=(N_SETS,),
        in_specs=[
            pl.BlockSpec((rows, LANES), lambda q: (0, q)),
            pl.BlockSpec((nbatch, SET_STATE), lambda q: (0, q)),
            pl.BlockSpec((nbatch, SET_STATE), lambda q: (0, q)),
            set_spec(tl, tl), set_spec(tl, SET_STATE), set_spec(tl, SET_STATE),
            set_spec(SET_STATE, tl), set_spec(SET_STATE, tl),
            set_spec(1, SET_STATE), set_spec(1, SET_STATE), set_spec(1, tl),
        ],
        out_specs=[
            pl.BlockSpec((rows, LANES), lambda q: (0, q)),
            pl.BlockSpec((nbatch, SET_STATE), lambda q: (0, q)),
            pl.BlockSpec((nbatch, SET_STATE), lambda q: (0, q)),
        ],
        out_shape=[
            jax.ShapeDtypeStruct((rows, SSM_WIDTH), F32),
            jax.ShapeDtypeStruct((nbatch, N_GROUPS * STATE), F32),
            jax.ShapeDtypeStruct((nbatch, N_GROUPS * STATE), F32),
        ],
        compiler_params=pltpu.CompilerParams(
            dimension_semantics=("arbitrary",), vmem_limit_bytes=VMEM_LIMIT),
        name="s5_sample",
    )(u, h0_re, h0_im, toep, wsr, wsi, wir, wii, ar, ai, d)


def _mix_out(x, attn, y, wglu_ref, bglu_ref, gs_ref, wout_ref, nf_ref):
    s = jax.nn.gelu(y)
    s = s * jax.nn.sigmoid(_dot(s.astype(BF16), wglu_ref[...]) + bglu_ref[...])
    sn = _rms(s, gs_ref[...]).astype(BF16)
    x1 = x + _dot(attn, wout_ref[:ATTN_WIDTH, :]) + _dot(sn, wout_ref[ATTN_WIDTH:, :])
    return x1, _rms(x1, nf_ref[...]).astype(BF16)


def _finish(acc, nfin_ref):
    return acc if nfin_ref is None else _rms(acc, nfin_ref[...])


def _ffn_prompt_kernel(*refs, tq, final):
    if final:
        (x_ref, attn_ref, y_ref, wglu_ref, bglu_ref, gs_ref, wout_ref, nf_ref, wup_ref, cw_ref, cb_ref,
         wdn_ref, nfin_ref, xo_ref, tail_ref, carry, act_scr) = refs
    else:
        (x_ref, attn_ref, y_ref, wglu_ref, bglu_ref, gs_ref, wout_ref, nf_ref, wup_ref, cw_ref, cb_ref,
         wdn_ref, xo_ref, tail_ref, carry, act_scr) = refs
        nfin_ref = None
    i = pl.program_id(1)

    @pl.when(i == 0)
    def _():
        carry[...] = jnp.zeros_like(carry)

    acc, hn = _mix_out(x_ref[0], attn_ref[0], y_ref[0], wglu_ref, bglu_ref, gs_ref, wout_ref, nf_ref)

    def conv(z, c0, cw):
        row = lax.broadcasted_iota(jnp.int32, z.shape, 0)
        prev1 = carry[SUBLANES - 1:SUBLANES, c0:c0 + cw]
        prev2 = carry[SUBLANES - 2:SUBLANES - 1, c0:c0 + cw]
        s1 = jnp.where(row == 0, prev1, pltpu.roll(z, 1, axis=0))
        s2 = jnp.where(row == 0, prev2, jnp.where(row == 1, prev1, pltpu.roll(z, 2, axis=0)))
        carry[:, c0:c0 + cw] = z[tq - SUBLANES:, :]
        return (cb_ref[:, c0:c0 + cw] + cw_ref[0:1, c0:c0 + cw] * s2
                + cw_ref[1:2, c0:c0 + cw] * s1 + cw_ref[2:3, c0:c0 + cw] * z)

    for c0, cw in FFN_CHUNKS:
        a = conv(_dot(hn, wup_ref[:, c0:c0 + cw]), c0, cw)
        g = conv(_dot(hn, wup_ref[:, D_FF + c0:D_FF + c0 + cw]), D_FF + c0, cw)
        act_scr[:, c0:c0 + cw] = (jax.nn.silu(g) * a).astype(BF16)
    xo_ref[0] = _finish(acc + _dot(act_scr[...], wdn_ref[...]), nfin_ref)

    @pl.when(i == pl.num_programs(1) - 1)
    def _():
        tail_ref[0] = carry[...]


def _ffn_weight_specs(final):
    specs = [
        _const_spec((SSM_WIDTH, SSM_WIDTH)), _const_spec((1, SSM_WIDTH)), _const_spec((1, SSM_WIDTH)),
        _const_spec((D_MODEL, D_MODEL)), _const_spec((1, D_MODEL)),
        _const_spec((D_MODEL, 2 * D_FF)), _const_spec((CONV_W, 2 * D_FF)), _const_spec((1, 2 * D_FF)),
        _const_spec((D_FF, D_MODEL)),
    ]
    if final:
        specs.append(_const_spec((1, D_MODEL)))
    return specs


def _ffn_weights(p, nfin):
    w = [p['w_glu'], p['b_glu'], p['g_ssm'], p['w_out'], p['norm_ffn'], p['w_up'], p['conv_w'],
         p['conv_b'], p['w_down']]
    if nfin is not None:
        w.append(nfin)
    return w


def _ffn_prompt(x, attn, y, p, nfin, tq):
    b, L, _ = x.shape
    final = nfin is not None
    tile = lambda w: pl.BlockSpec((1, tq, w), lambda bi, i: (bi, i, 0))
    return pl.pallas_call(
        functools.partial(_ffn_prompt_kernel, tq=tq, final=final),
        grid=(b, L // tq),
        in_specs=[tile(D_MODEL), tile(ATTN_WIDTH), tile(SSM_WIDTH)] + _ffn_weight_specs(final),
        out_specs=[tile(D_MODEL), pl.BlockSpec((1, SUBLANES, 2 * D_FF), lambda bi, i: (bi, 0, 0))],
        out_shape=[jax.ShapeDtypeStruct((b, L, D_MODEL), F32),
                   jax.ShapeDtypeStruct((b, SUBLANES, 2 * D_FF), F32)],
        scratch_shapes=[pltpu.VMEM((SUBLANES, 2 * D_FF), F32), pltpu.VMEM((tq, D_FF), BF16)],
        compiler_params=pltpu.CompilerParams(
            dimension_semantics=("arbitrary", "arbitrary"), vmem_limit_bytes=VMEM_LIMIT),
        name="ffn_prompt",
    )(x, attn, y, *_ffn_weights(p, nfin))


def _ffn_sample_kernel(*refs, nbatch, t_new, final):
    if final:
        (x_ref, attn_ref, y_ref, prev_ref, wglu_ref, bglu_ref, gs_ref, wout_ref, nf_ref, wup_ref, cw_ref,
         cb_ref, wdn_ref, nfin_ref, xo_ref, tail_ref, act_scr) = refs
    else:
        (x_ref, attn_ref, y_ref, prev_ref, wglu_ref, bglu_ref, gs_ref, wout_ref, nf_ref, wup_ref, cw_ref,
         cb_ref, wdn_ref, xo_ref, tail_ref, act_scr) = refs
        nfin_ref = None
    acc, hn = _mix_out(x_ref[...], attn_ref[...], y_ref[...], wglu_ref, bglu_ref, gs_ref, wout_ref, nf_ref)

    def conv(z, c0, cw):
        buf = [prev_ref[0, :, c0:c0 + cw], prev_ref[1, :, c0:c0 + cw]]
        buf += [z[t * nbatch:(t + 1) * nbatch] for t in range(t_new)]
        for k in range(CONV_W - 1):
            tail_ref[k, :, c0:c0 + cw] = buf[t_new + k]
        w = [cw_ref[k:k + 1, c0:c0 + cw] for k in range(CONV_W)]
        bias = cb_ref[:, c0:c0 + cw]
        return jnp.concatenate(
            [bias + w[0] * buf[t] + w[1] * buf[t + 1] + w[2] * buf[t + 2] for t in range(t_new)], axis=0)

    for c0, cw in FFN_CHUNKS:
        a = conv(_dot(hn, wup_ref[:, c0:c0 + cw]), c0, cw)
        g = conv(_dot(hn, wup_ref[:, D_FF + c0:D_FF + c0 + cw]), D_FF + c0, cw)
        act_scr[:, c0:c0 + cw] = (jax.nn.silu(g) * a).astype(BF16)
    xo_ref[...] = _finish(acc + _dot(act_scr[...], wdn_ref[...]), nfin_ref)


def _ffn_sample(x, attn, y, prev, p, nfin, nbatch, t_new):
    rows = x.shape[0]
    final = nfin is not None
    full = lambda shape: pl.BlockSpec(shape, lambda i: (0,) * len(shape))
    return pl.pallas_call(
        functools.partial(_ffn_sample_kernel, nbatch=nbatch, t_new=t_new, final=final),
        grid=(1,),
        in_specs=[full((rows, D_MODEL)), full((rows, ATTN_WIDTH)), full((rows, SSM_WIDTH)),
                  full((CONV_W - 1, nbatch, 2 * D_FF))] + _ffn_weight_specs(final),
        out_specs=[full((rows, D_MODEL)), full((CONV_W - 1, nbatch, 2 * D_FF))],
        out_shape=[jax.ShapeDtypeStruct((rows, D_MODEL), F32),
                   jax.ShapeDtypeStruct((CONV_W - 1, nbatch, 2 * D_FF), F32)],
        scratch_shapes=[pltpu.VMEM((rows, D_FF), BF16)],
        compiler_params=pltpu.CompilerParams(
            dimension_semantics=("arbitrary",), vmem_limit_bytes=VMEM_LIMIT),
        name="ffn_sample",
    )(x, attn, y, prev, *_ffn_weights(p, nfin))


def _to_token_major(a, nbatch, t_new):
    w = a.shape[-1]
    return a.reshape(nbatch, SUBLANES, w)[:, :t_new].transpose(1, 0, 2).reshape(t_new * nbatch, w)


def _to_batch_major(a, nbatch, t_new):
    w = a.shape[-1]
    a = a.reshape(t_new, nbatch, w).transpose(1, 0, 2)
    return jnp.pad(a, ((0, 0), (0, SUBLANES - t_new), (0, 0))).reshape(nbatch * SUBLANES, w)


def kernel(x_prompt, x_sample, cache_k_win, cache_v_win, state_ssm_re, state_ssm_im, state_conv, norm_mix, w_in, sinks, lam_re, lam_im, log_step, b_re, b_im, c_re, c_im, d_skip, w_glu, b_glu, g_attn, g_ssm, w_out, norm_ffn, w_up, conv_w, conv_b, w_down, norm_final):
    depth = w_in.shape[0]
    bp, L, _ = x_prompt.shape
    nbatch, t_new, _ = x_sample.shape
    win = cache_k_win.shape[2]
    assert win == WINDOW and L % WINDOW == 0 and L >= WINDOW and t_new <= SUBLANES // 2
    tq_in = min(512, L)
    tq_ffn = min(512, L)
    bb = min(16, nbatch)

    xp = x_prompt
    xs = jnp.pad(x_sample, ((0, 0), (0, SUBLANES - t_new), (0, 0))).reshape(nbatch * SUBLANES, D_MODEL)
    nfin = norm_final.reshape(1, D_MODEL)
    outs_p, outs_s = [], []
    for l in range(depth):
        last = l == depth - 1
        p = dict(
            sinks=sinks[l], norm_mix=norm_mix[l].reshape(1, -1), w_in=w_in[l].astype(BF16),
            g_attn=g_attn[l].reshape(1, -1), w_glu=w_glu[l].astype(BF16), b_glu=b_glu[l].reshape(1, -1),
            g_ssm=g_ssm[l].reshape(1, -1), w_out=w_out[l].astype(BF16), norm_ffn=norm_ffn[l].reshape(1, -1),
            w_up=w_up[l].astype(BF16), conv_w=conv_w[l], conv_b=conv_b[l].reshape(1, -1),
            w_down=w_down[l].astype(BF16))
        ops = _s5_operators(lam_re[l], lam_im[l], log_step[l], b_re[l], b_im[l], c_re[l], c_im[l], d_skip[l])
        pw_re, pw_im = _chunk_decay_powers(lam_re[l], lam_im[l], log_step[l])

        attn, u, kwin, vwin = _mix_in_prompt(xp, p, tq_in)
        y, hre, him = _s5_prompt(u, ops, pw_re, pw_im)
        xp, tail = _ffn_prompt(xp, attn, y, p, nfin if last else None, tq_ffn)
        outs_p.append((kwin.reshape(bp, WINDOW, N_KV_HEADS, HEAD_DIM),
                       vwin.reshape(bp, WINDOW, N_KV_HEADS, HEAD_DIM),
                       hre.reshape(bp, N_GROUPS, STATE), him.reshape(bp, N_GROUPS, STATE),
                       tail[:, SUBLANES - (CONV_W - 1):]))

        kc = cache_k_win[l].reshape(nbatch, WINDOW, KV_WIDTH)
        vc = cache_v_win[l].reshape(nbatch, WINDOW, KV_WIDTH)
        attn_s, u_s, kwin_s, vwin_s = _mix_in_sample(xs, p, kc, vc, bb, t_new)
        y_s, hre_s, him_s = _s5_sample(u_s, state_ssm_re[l].reshape(nbatch, -1),
                                       state_ssm_im[l].reshape(nbatch, -1), ops, t_new)
        tm = functools.partial(_to_token_major, nbatch=nbatch, t_new=t_new)
        xs_tm, tail_s = _ffn_sample(tm(xs), tm(attn_s), tm(y_s), state_conv[l].transpose(1, 0, 2), p,
                                    nfin if last else None, nbatch, t_new)
        xs = _to_batch_major(xs_tm, nbatch, t_new)
        outs_s.append((kwin_s.reshape(nbatch, WINDOW, N_KV_HEADS, HEAD_DIM),
                       vwin_s.reshape(nbatch, WINDOW, N_KV_HEADS, HEAD_DIM),
                       hre_s.reshape(nbatch, N_GROUPS, STATE), him_s.reshape(nbatch, N_GROUPS, STATE),
                       tail_s.transpose(1, 0, 2)))

    y_sample = xs_tm.reshape(t_new, nbatch, D_MODEL).transpose(1, 0, 2)
    stack = lambda outs: tuple(jnp.stack(z) for z in zip(*outs))
    return (xp, y_sample) + stack(outs_p) + stack(outs_s)
```
